```python
import math
import jax, jax.numpy as jnp
from jax import lax
import numpy as np

D_MODEL = 1024
BATCH = 16
SEQ = 4096
DEPTH = 2
DEC_BATCH = 8
DEC_SEQ = 2048
PAST_LEN = 128

N_BRANCH = 4
BRANCH_W = D_MODEL // 4
N_GROUPS = 4
GROUP_DIM = BRANCH_W // N_GROUPS
D_FF = 2816
CHUNK = 128
POOL_WINDOWS = (2, 4, 8, 16)
CONV_WIDTH = 3
LN_EPS = 1e-5
ALPHA = (2.0 * DEPTH) ** 0.25
BETA = (8.0 * DEPTH) ** -0.25
IN_SPLITS = (BRANCH_W, 2 * BRANCH_W, 3 * BRANCH_W, 4 * BRANCH_W, 6 * BRANCH_W)
IN_WIDTH = 7 * BRANCH_W

kernel_name = "hybrid_gated_parallel_encoder"


def _layer_norm(x, g, b):
    xf = x.astype(jnp.float32)
    mu = jnp.mean(xf, axis=-1, keepdims=True)
    var = jnp.mean(jnp.square(xf - mu), axis=-1, keepdims=True)
    y = (xf - mu) * lax.rsqrt(var + LN_EPS)
    return (y * g.astype(jnp.float32) + b.astype(jnp.float32)).astype(x.dtype)


def _swiglu(x, w1, w3, w2):
    return (jax.nn.silu(x @ w1) * (x @ w3)) @ w2


def _short_conv(h, w):
    hp = jnp.pad(h, ((0, 0), (1, 1), (0, 0)))
    return hp[:, :-2] * w[0] + hp[:, 1:-1] * w[1] + hp[:, 2:] * w[2]


def _fourier(z):
    b, s, _ = z.shape
    zh = z.reshape(b, s, N_GROUPS, GROUP_DIM).astype(jnp.float32)
    f = jnp.fft.fft2(zh, axes=(1, 3), norm="ortho").real
    return f.reshape(b, s, BRANCH_W).astype(z.dtype)


def _spatial_gate(uv, g, bn, w_s, b_s):
    u, v = jnp.split(uv, 2, axis=-1)
    v = _layer_norm(v, g, bn)
    b, s, _ = v.shape
    vc = v.reshape(b, s // CHUNK, CHUNK, N_GROUPS, GROUP_DIM)
    mixed = jnp.einsum("hpq,bnqhc->bnphc", w_s, vc) + b_s.T[:, :, None]
    return u * mixed.reshape(b, s, BRANCH_W)


def _multiscale_pool(z, w_grp, scale):
    b, s, _ = z.shape
    zf = z.reshape(b, s, N_GROUPS, GROUP_DIM).astype(jnp.float32)
    csum = jnp.pad(jnp.cumsum(zf, axis=1), ((0, 0), (1, 0), (0, 0), (0, 0)))
    t = jnp.arange(s)
    outs = []
    for gi, win in enumerate(POOL_WINDOWS):
        half = win // 2
        hi = jnp.minimum(t + half, s)
        lo = jnp.maximum(t - half, 0)
        cg = csum[:, :, gi]
        tot = jnp.take(cg, hi, axis=1) - jnp.take(cg, lo, axis=1)
        mean = tot / (hi - lo).astype(jnp.float32)[:, None]
        outs.append(mean - zf[:, :, gi])
    pooled = jnp.stack(outs, axis=2).astype(z.dtype)
    mixed = jnp.einsum("bsgc,gcd->bsgd", pooled, w_grp)
    return mixed.reshape(b, s, BRANCH_W) * scale


def _token_mix(x, w_in, conv_w, sgu_ln_g, sgu_ln_b, sgu_w, sgu_b, pool_w, pool_scale,
               w_gate, b_gate, w_branch, w_out):
    z = x @ w_in
    h, gb, gc, zf, uv, zp = jnp.split(z, IN_SPLITS, axis=-1)
    branches = (
        gb * _short_conv(gc * h, conv_w),
        _fourier(zf),
        _spatial_gate(jax.nn.gelu(uv), sgu_ln_g, sgu_ln_b, sgu_w, sgu_b),
        _multiscale_pool(zp, pool_w, pool_scale),
    )
    merged = jnp.zeros_like(x)
    for i, br in enumerate(branches):
        gate = jax.nn.sigmoid(x @ w_gate[i] + b_gate[i])
        merged = merged + gate * (br @ w_branch[i])
    return merged @ w_out


def _trunk(x, ln_g, ln_b, ffn_w1, ffn_w3, ffn_w2, w_in, conv_w, sgu_ln_g, sgu_ln_b, sgu_w,
           sgu_b, pool_w, pool_scale, w_gate, b_gate, w_branch, w_out):
    for l in range(DEPTH):
        f1 = _swiglu(x, ffn_w1[l, 0], ffn_w3[l, 0], ffn_w2[l, 0])
        x = _layer_norm(ALPHA * x + 0.5 * f1, ln_g[l, 0], ln_b[l, 0])
        m = _token_mix(x, w_in[l], conv_w[l], sgu_ln_g[l], sgu_ln_b[l], sgu_w[l], sgu_b[l],
                       pool_w[l], pool_scale[l], w_gate[l], b_gate[l], w_branch[l], w_out[l])
        x = _layer_norm(ALPHA * x + m, ln_g[l, 1], ln_b[l, 1])
        f2 = _swiglu(x, ffn_w1[l, 1], ffn_w3[l, 1], ffn_w2[l, 1])
        x = _layer_norm(ALPHA * x + 0.5 * f2, ln_g[l, 2], ln_b[l, 2])
    return x


def setup_inputs(seed: int = 0) -> dict:
    key = jax.random.key(seed)
    ks = jax.random.split(key, 20)
    f32 = jnp.float32
    nrm = lambda k, shape, s: jax.random.normal(k, shape, f32) * s
    return {
        "x_prompt": nrm(ks[0], (BATCH, SEQ, D_MODEL), 1.0),
        "x_sample": nrm(ks[1], (DEC_BATCH, DEC_SEQ, D_MODEL), 1.0),
        "ln_g": 1.0 + nrm(ks[2], (DEPTH, 3, D_MODEL), 0.01),
        "ln_b": nrm(ks[3], (DEPTH, 3, D_MODEL), 0.01),
        "ffn_w1": nrm(ks[4], (DEPTH, 2, D_MODEL, D_FF), D_MODEL ** -0.5),
        "ffn_w3": nrm(ks[5], (DEPTH, 2, D_MODEL, D_FF), D_MODEL ** -0.5),
        "ffn_w2": nrm(ks[6], (DEPTH, 2, D_FF, D_MODEL), BETA * D_FF ** -0.5),
        "w_in": nrm(ks[7], (DEPTH, D_MODEL, IN_WIDTH), D_MODEL ** -0.5),
        "conv_w": nrm(ks[8], (DEPTH, CONV_WIDTH, BRANCH_W), CONV_WIDTH ** -0.5),
        "sgu_ln_g": 1.0 + nrm(ks[9], (DEPTH, BRANCH_W), 0.01),
        "sgu_ln_b": nrm(ks[10], (DEPTH, BRANCH_W), 0.01),
        "sgu_w": nrm(ks[11], (DEPTH, N_GROUPS, CHUNK, CHUNK), CHUNK ** -0.5),
        "sgu_b": 1.0 + nrm(ks[12], (DEPTH, N_GROUPS, CHUNK), 0.01),
        "pool_w": nrm(ks[13], (DEPTH, N_GROUPS, GROUP_DIM, GROUP_DIM), GROUP_DIM ** -0.5),
        "pool_scale": 1.0 + nrm(ks[14], (DEPTH, BRANCH_W), 0.01),
        "w_gate": nrm(ks[15], (DEPTH, N_BRANCH, D_MODEL, D_MODEL), D_MODEL ** -0.5),
        "b_gate": nrm(ks[16], (DEPTH, N_BRANCH, D_MODEL), 0.01),
        "w_branch": nrm(ks[17], (DEPTH, N_BRANCH, BRANCH_W, D_MODEL), BETA * BRANCH_W ** -0.5),
        "w_out": nrm(ks[18], (DEPTH, D_MODEL, D_MODEL), BETA * D_MODEL ** -0.5),
    }


def reference(x_prompt, x_sample, ln_g, ln_b, ffn_w1, ffn_w3, ffn_w2, w_in, conv_w, sgu_ln_g,
              sgu_ln_b, sgu_w, sgu_b, pool_w, pool_scale, w_gate, b_gate, w_branch, w_out):
    y_prompt = _trunk(x_prompt, ln_g, ln_b, ffn_w1, ffn_w3, ffn_w2, w_in, conv_w, sgu_ln_g,
                      sgu_ln_b, sgu_w, sgu_b, pool_w, pool_scale, w_gate, b_gate, w_branch, w_out)
    y_sample = _trunk(x_sample, ln_g, ln_b, ffn_w1, ffn_w3, ffn_w2, w_in, conv_w, sgu_ln_g,
                      sgu_ln_b, sgu_w, sgu_b, pool_w, pool_scale, w_gate, b_gate, w_branch, w_out)
    return (y_prompt, y_sample)
```

```python
import functools
import math

import jax
import jax.numpy as jnp
from jax.experimental import pallas as pl
from jax.experimental.pallas import tpu as pltpu

D_MODEL = 1024
DEPTH = 2
BRANCH_W = D_MODEL // 4
N_GROUPS = 4
GROUP_DIM = BRANCH_W // N_GROUPS
D_FF = 2816
CHUNK = 128
POOL_WINDOWS = (2, 4, 8, 16)
LN_EPS = 1e-5
ALPHA = (2.0 * DEPTH) ** 0.25
IN_WIDTH = 7 * BRANCH_W

V7X_BF16_SUBLANES = 16
V7X_VMEM_LIMIT_BYTES = 56 * 1024 * 1024

HALO = V7X_BF16_SUBLANES
POOL_HALF_MAX = max(POOL_WINDOWS) // 2
assert POOL_HALF_MAX <= HALO

TM_FFN = 512
FF_CHUNK = 1408
TM_MIX = 512
DFT_BLOCK = 1024

BF16 = jnp.bfloat16
F32 = jnp.float32


def _dot(a, b):
    return jnp.dot(a, b, preferred_element_type=F32)


def _layer_norm(x, g, b):
    mu = jnp.mean(x, axis=-1, keepdims=True)
    xc = x - mu
    var = jnp.mean(xc * xc, axis=-1, keepdims=True)
    return xc * jax.lax.rsqrt(var + LN_EPS) * g + b


def _resident(shape):
    zeros = (0,) * len(shape)
    return pl.BlockSpec(shape, lambda *_: zeros, pipeline_mode=pl.Buffered(1))


def _params(semantics):
    return pltpu.CompilerParams(dimension_semantics=semantics,
                                vmem_limit_bytes=V7X_VMEM_LIMIT_BYTES)


def _ffn_kernel(x_ref, w1_ref, w3_ref, w2_ref, g_ref, b_ref, o_ref):
    x = x_ref[...]
    xb = x.astype(BF16)
    acc = jnp.zeros(x.shape, F32)
    for c in range(D_FF // FF_CHUNK):
        cols = slice(c * FF_CHUNK, (c + 1) * FF_CHUNK)
        a = _dot(xb, w1_ref[:, cols])
        gate = _dot(xb, w3_ref[:, cols])
        h = (a * jax.nn.sigmoid(a) * gate).astype(BF16)
        acc = acc + _dot(h, w2_ref[cols, :])
    o_ref[...] = _layer_norm(ALPHA * x + 0.5 * acc, g_ref[...], b_ref[...])


def _ffn(x2d, w1, w3, w2, g, b):
    t = x2d.shape[0]
    row = pl.BlockSpec((TM_FFN, D_MODEL), lambda i: (i, 0))
    return pl.pallas_call(
        _ffn_kernel,
        grid=(t // TM_FFN,),
        in_specs=[row, _resident((D_MODEL, D_FF)), _resident((D_MODEL, D_FF)),
                  _resident((D_FF, D_MODEL)), _resident((1, D_MODEL)), _resident((1, D_MODEL))],
        out_specs=row,
        out_shape=jax.ShapeDtypeStruct((t, D_MODEL), F32),
        compiler_params=_params(("parallel",)),
        name="ffn",
    )(x2d, w1, w3, w2, g, b)


def _gelu_tanh(x):
    c = math.sqrt(2.0 / math.pi)
    return 0.5 * x * (1.0 + jnp.tanh(c * (x + 0.044715 * (x * x * x))))


def _mix_in_kernel(seq, x_ref, xp_ref, xn_ref, win_ref, convw_ref, sgug_ref, sgub_ref,
                   sguw_ref, sgubias_ref, poolw_ref, poolscale_ref, chdft_ref,
                   br_ref, zc_ref, zs_ref, xe_ref, g_scr, zp_scr):
    tm = TM_MIX
    tiles_per_seq = seq // tm
    j = pl.program_id(0) % tiles_per_seq
    keep_prev = (j > 0).astype(F32)
    keep_next = (j < tiles_per_seq - 1).astype(F32)
    xe_ref[0:HALO, :] = (xp_ref[...] * keep_prev).astype(BF16)
    xe_ref[HALO:HALO + tm, :] = x_ref[...].astype(BF16)
    xe_ref[HALO + tm:, :] = (xn_ref[...] * keep_next).astype(BF16)

    z = _dot(xe_ref[...], win_ref[...])
    w = BRANCH_W
    body = slice(HALO, HALO + tm)

    g_scr[...] = z[:, 2 * w:3 * w] * z[:, 0:w]
    cw = convw_ref[...]
    conv = (g_scr[pl.ds(HALO - 1, tm), :] * cw[0:1, :]
            + g_scr[pl.ds(HALO, tm), :] * cw[1:2, :]
            + g_scr[pl.ds(HALO + 1, tm), :] * cw[2:3, :])
    br_ref[:, 0:w] = (z[body, w:2 * w] * conv).astype(BF16)

    uv = _gelu_tanh(z[body, 4 * w:6 * w])
    u = uv[:, 0:w]
    v = _layer_norm(uv[:, w:2 * w], sgug_ref[...], sgub_ref[...]).astype(BF16)
    lane_head = jax.lax.broadcasted_iota(jnp.int32, (CHUNK, w), 1) // GROUP_DIM
    zero = jnp.zeros((CHUNK, w), BF16)
    for c in range(tm // CHUNK):
        rows = slice(c * CHUNK, (c + 1) * CHUNK)
        vc = v[rows, :]
        vm = jnp.concatenate([jnp.where(lane_head == hd, vc, zero) for hd in range(N_GROUPS)], axis=0)
        mixed = _dot(sguw_ref[...], vm) + sgubias_ref[...]
        br_ref[rows, w:2 * w] = (u[rows, :] * mixed).astype(BF16)

    zp_scr[...] = z[:, 6 * w:7 * w]
    pos = j * tm + jax.lax.broadcasted_iota(jnp.int32, (tm, 1), 0)
    lane_group = jax.lax.broadcasted_iota(jnp.int32, (tm, w), 1) // GROUP_DIM
    total = jnp.zeros((tm, w), F32)
    mean = jnp.zeros((tm, w), F32)
    done = 0
    for gi, win in enumerate(POOL_WINDOWS):
        half = win // 2
        for off in list(range(-half, -done)) + list(range(done, half)):
            total = total + zp_scr[pl.ds(HALO + off, tm), :]
        done = half
        count = (jnp.minimum(pos + half, seq) - jnp.maximum(pos - half, 0)).astype(F32)
        mean = jnp.where(lane_group == gi, total / count, mean)
    pooled = (mean - zp_scr[pl.ds(HALO, tm), :]).astype(BF16)
    br_ref[:, 2 * w:3 * w] = (_dot(pooled, poolw_ref[...]) * poolscale_ref[...]).astype(BF16)

    zcs = _dot(z[body, 3 * w:4 * w].astype(BF16), chdft_ref[...])
    zc_ref[...] = zcs[:, 0:w].astype(BF16)
    zs_ref[...] = zcs[:, w:2 * w].astype(BF16)


def _mix_in(x2d, batch, seq, win, convw, sgug, sgub, sguw, sgubias, poolw, poolscale, chdft):
    t = x2d.shape[0]
    tm = TM_MIX
    w = BRANCH_W
    tiles_per_seq = seq // tm
    halo_blocks = t // HALO
    per_tile = tm // HALO
    row = pl.BlockSpec((tm, D_MODEL), lambda i: (i, 0))
    prev = pl.BlockSpec((HALO, D_MODEL), lambda i: (jnp.maximum(i * per_tile - 1, 0), 0))
    nxt = pl.BlockSpec((HALO, D_MODEL),
                       lambda i: (jnp.minimum((i + 1) * per_tile, halo_blocks - 1), 0))
    seq_major = pl.BlockSpec((tm, w), lambda i: (i % tiles_per_seq, i // tiles_per_seq))
    return pl.pallas_call(
        functools.partial(_mix_in_kernel, seq),
        grid=(t // tm,),
        in_specs=[row, prev, nxt, _resident((D_MODEL, IN_WIDTH)), _resident((3, w)),
                  _resident((1, w)), _resident((1, w)), _resident((CHUNK, N_GROUPS * CHUNK)),
                  _resident((CHUNK, w)), _resident((w, w)), _resident((1, w)),
                  _resident((w, 2 * w))],
        out_specs=[pl.BlockSpec((tm, 3 * w), lambda i: (i, 0)), seq_major, seq_major],
        out_shape=[jax.ShapeDtypeStruct((t, 3 * w), BF16),
                   jax.ShapeDtypeStruct((seq, batch * w), BF16),
                   jax.ShapeDtypeStruct((seq, batch * w), BF16)],
        scratch_shapes=[pltpu.VMEM((tm + 2 * HALO, D_MODEL), BF16),
                        pltpu.VMEM((tm + 2 * HALO, w), F32),
                        pltpu.VMEM((tm + 2 * HALO, w), F32)],
        compiler_params=_params(("parallel",)),
        name="mix_in",
    )(x2d, x2d, x2d, win, convw, sgug, sgub, sguw, sgubias, poolw, poolscale, chdft)


def _seq_dft_kernel(scale, cos_ref, nsin_ref, zc_ref, zs_ref, o_ref, acc_ref):
    k = pl.program_id(2)

    @pl.when(k == 0)
    def _():
        acc_ref[...] = jnp.zeros(acc_ref.shape, F32)

    acc_ref[...] += _dot(cos_ref[...], zc_ref[...]) + _dot(nsin_ref[...], zs_ref[...])

    @pl.when(k == pl.num_programs(2) - 1)
    def _():
        o_ref[...] = (acc_ref[...] * scale).astype(BF16)


def _seq_dft(cos_m, nsin_m, zc, zs):
    seq, n = zc.shape
    bm = min(DFT_BLOCK, seq)
    bn = min(DFT_BLOCK, n)
    bk = min(DFT_BLOCK, seq)
    lhs = pl.BlockSpec((bm, bk), lambda i, j, k: (i, k))
    rhs = pl.BlockSpec((bk, bn), lambda i, j, k: (k, j))
    return pl.pallas_call(
        functools.partial(_seq_dft_kernel, 1.0 / math.sqrt(seq)),
        grid=(seq // bm, n // bn, seq // bk),
        in_specs=[lhs, lhs, rhs, rhs],
        out_specs=pl.BlockSpec((bm, bn), lambda i, j, k: (i, j)),
        out_shape=jax.ShapeDtypeStruct((seq, n), BF16),
        scratch_shapes=[pltpu.VMEM((bm, bn), F32)],
        compiler_params=_params(("parallel", "parallel", "arbitrary")),
        name="seq_dft",
    )(cos_m, nsin_m, zc, zs)


def _mix_out_kernel(x_ref, br_ref, four_ref, wg_ref, bg_ref, wb_ref, wo_ref, g_ref, b_ref, o_ref):
    x = x_ref[...]
    xb = x.astype(BF16)
    w = BRANCH_W
    branches = (br_ref[:, 0:w], four_ref[...], br_ref[:, w:2 * w], br_ref[:, 2 * w:3 * w])
    merged = jnp.zeros(x.shape, F32)
    for i, br in enumerate(branches):
        gate = jax.nn.sigmoid(_dot(xb, wg_ref[i]) + bg_ref[i])
        merged = merged + gate * _dot(br, wb_ref[i])
    mix = _dot(merged.astype(BF16), wo_ref[...])
    o_ref[...] = _layer_norm(ALPHA * x + mix, g_ref[...], b_ref[...])


def _mix_out(x2d, br, four, seq, wg, bg, wb, wo, g, b):
    t = x2d.shape[0]
    tm = TM_MIX
    w = BRANCH_W
    tiles_per_seq = seq // tm
    row = pl.BlockSpec((tm, D_MODEL), lambda i: (i, 0))
    return pl.pallas_call(
        _mix_out_kernel,
        grid=(t // tm,),
        in_specs=[row, pl.BlockSpec((tm, 3 * w), lambda i: (i, 0)),
                  pl.BlockSpec((tm, w), lambda i: (i % tiles_per_seq, i // tiles_per_seq)),
                  _resident((4, D_MODEL, D_MODEL)), _resident((4, 1, D_MODEL)),
                  _resident((4, w, D_MODEL)), _resident((D_MODEL, D_MODEL)),
                  _resident((1, D_MODEL)), _resident((1, D_MODEL))],
        out_specs=row,
        out_shape=jax.ShapeDtypeStruct((t, D_MODEL), F32),
        compiler_params=_params(("parallel",)),
        name="mix_out",
    )(x2d, br, four, wg, bg, wb, wo, g, b)


def _block_diag(blocks):
    g, a, b = blocks.shape
    eye = jnp.eye(g, dtype=blocks.dtype)
    return (eye[:, None, :, None] * blocks[:, :, None, :]).reshape(g * a, g * b)


def _channel_dft():
    c = jnp.arange(GROUP_DIM, dtype=jnp.int32)
    ang = (2.0 * math.pi / GROUP_DIM) * ((c[:, None] * c[None, :]) % GROUP_DIM).astype(F32)
    norm = 1.0 / math.sqrt(GROUP_DIM)
    cos_b = jnp.broadcast_to(jnp.cos(ang) * norm, (N_GROUPS, GROUP_DIM, GROUP_DIM))
    sin_b = jnp.broadcast_to(jnp.sin(ang) * norm, (N_GROUPS, GROUP_DIM, GROUP_DIM))
    return jnp.concatenate([_block_diag(cos_b), _block_diag(sin_b)], axis=1).astype(BF16)


def _seq_dft_mats(seq):
    lo = 64
    hi = seq // lo
    k = jnp.arange(seq, dtype=jnp.int32)[:, None]
    s_hi = jnp.arange(hi, dtype=jnp.int32)[None, :]
    s_lo = jnp.arange(lo, dtype=jnp.int32)[None, :]
    unit = 2.0 * math.pi / seq
    ang_hi = unit * ((k * s_hi * lo) % seq).astype(F32)
    ang_lo = unit * ((k * s_lo) % seq).astype(F32)
    ch, sh = jnp.cos(ang_hi)[:, :, None], jnp.sin(ang_hi)[:, :, None]
    cl, sl = jnp.cos(ang_lo)[:, None, :], jnp.sin(ang_lo)[:, None, :]
    cos_m = (ch * cl - sh * sl).reshape(seq, seq).astype(BF16)
    nsin_m = (-(sh * cl + ch * sl)).reshape(seq, seq).astype(BF16)
    return cos_m, nsin_m


def _trunk(x, p):
    batch, seq, _ = x.shape
    cos_m, nsin_m = _seq_dft_mats(seq)
    x2d = x.reshape(batch * seq, D_MODEL)
    for l in range(DEPTH):
        x2d = _ffn(x2d, p["w1"][l][0], p["w3"][l][0], p["w2"][l][0], p["ln_g"][l][0], p["ln_b"][l][0])
        br, zc, zs = _mix_in(x2d, batch, seq, p["w_in"][l], p["conv_w"][l], p["sgu_ln_g"][l],
                             p["sgu_ln_b"][l], p["sgu_w"][l], p["sgu_bias"][l], p["pool_w"][l],
                             p["pool_scale"][l], p["chdft"])
        four = _seq_dft(cos_m, nsin_m, zc, zs)
        x2d = _mix_out(x2d, br, four, seq, p["w_gate"][l], p["b_gate"][l], p["w_branch"][l],
                       p["w_out"][l], p["ln_g"][l][1], p["ln_b"][l][1])
        x2d = _ffn(x2d, p["w1"][l][1], p["w3"][l][1], p["w2"][l][1], p["ln_g"][l][2], p["ln_b"][l][2])
    return x2d.reshape(batch, seq, D_MODEL)


def kernel(x_prompt, x_sample, ln_g, ln_b, ffn_w1, ffn_w3, ffn_w2, w_in, conv_w, sgu_ln_g, sgu_ln_b, sgu_w, sgu_b, pool_w, pool_scale, w_gate, b_gate, w_branch, w_out):
    w = BRANCH_W
    p = {
        "ln_g": ln_g.reshape(DEPTH, 3, 1, D_MODEL),
        "ln_b": ln_b.reshape(DEPTH, 3, 1, D_MODEL),
        "w1": ffn_w1.astype(BF16),
        "w3": ffn_w3.astype(BF16),
        "w2": ffn_w2.astype(BF16),
        "w_in": w_in.astype(BF16),
        "conv_w": conv_w,
        "sgu_ln_g": sgu_ln_g.reshape(DEPTH, 1, w),
        "sgu_ln_b": sgu_ln_b.reshape(DEPTH, 1, w),
        "sgu_w": jnp.transpose(sgu_w, (0, 2, 1, 3)).reshape(DEPTH, CHUNK, N_GROUPS * CHUNK).astype(BF16),
        "sgu_bias": jnp.repeat(jnp.transpose(sgu_b, (0, 2, 1)), GROUP_DIM, axis=2),
        "pool_w": jax.vmap(_block_diag)(pool_w).astype(BF16),
        "pool_scale": pool_scale.reshape(DEPTH, 1, w),
        "w_gate": w_gate.astype(BF16),
        "b_gate": b_gate.reshape(DEPTH, 4, 1, D_MODEL),
        "w_branch": w_branch.astype(BF16),
        "w_out": w_out.astype(BF16),
        "chdft": _channel_dft(),
    }
    return _trunk(x_prompt, p), _trunk(x_sample, p)
```
